```python
import math
import jax, jax.numpy as jnp
from jax import lax
import numpy as np

D_MODEL = 1024
BATCH = 2
SEQ = 8192
DEPTH = 1
DEC_BATCH = 128
DEC_SEQ = 4
PAST_LEN = 8192
PAGE_SIZE = 128

N_HEADS = 16
N_KV_HEADS = 4
HEAD_DIM = D_MODEL // N_HEADS
GROUP = N_HEADS // N_KV_HEADS
ATTN_W = N_HEADS * HEAD_DIM
KV_W = N_KV_HEADS * HEAD_DIM
WINDOW = 128
BLOCK = WINDOW
CONV_CH = D_MODEL
CONV_WIDTH = 31
CONV_HIST = CONV_WIDTH - 1
EPS = 1e-6
SPLITS = (CONV_CH, CONV_CH, CONV_CH, ATTN_W, KV_W, KV_W, ATTN_W, D_MODEL, D_MODEL)
N_IN = sum(SPLITS)

kernel_name = "gated_conformer_swa_hybrid_step"


def rms_norm(x, g):
    xf = x.astype(jnp.float32)
    y = xf * lax.rsqrt(jnp.mean(xf * xf, axis=-1, keepdims=True) + EPS)
    return (y * g.astype(jnp.float32)).astype(x.dtype)


def layer_norm(x, g, b):
    xf = x.astype(jnp.float32)
    mu = jnp.mean(xf, axis=-1, keepdims=True)
    var = jnp.mean(jnp.square(xf - mu), axis=-1, keepdims=True)
    y = (xf - mu) * lax.rsqrt(var + EPS)
    return (y * g.astype(jnp.float32) + b.astype(jnp.float32)).astype(x.dtype)


def split_columns(z):
    points, acc = [], 0
    for w in SPLITS[:-1]:
        acc += w
        points.append(acc)
    return jnp.split(z, points, axis=-1)


def alibi_slopes():
    h = jnp.arange(1, N_HEADS + 1, dtype=jnp.float32)
    return jnp.exp2(-8.0 * h / N_HEADS).reshape(N_KV_HEADS, GROUP, 1, 1)


def sink_probs(scores, dist, valid, sinks):
    logits = jnp.where(valid, scores - alibi_slopes() * dist, -jnp.inf)
    sink = jnp.broadcast_to(sinks.astype(jnp.float32).reshape(N_KV_HEADS, GROUP, 1, 1),
                            logits.shape[:-1] + (1,))
    return jax.nn.softmax(jnp.concatenate([logits, sink], axis=-1), axis=-1)[..., :-1]


def attn_prompt(q, k, v, sinks):
    B, T = q.shape[0], q.shape[1]
    nb = T // BLOCK
    scale = 1.0 / math.sqrt(HEAD_DIM)
    qb = q.reshape(B, nb, BLOCK, N_KV_HEADS, GROUP, HEAD_DIM)
    kb = k.reshape(B, nb, BLOCK, N_KV_HEADS, HEAD_DIM)
    vb = v.reshape(B, nb, BLOCK, N_KV_HEADS, HEAD_DIM)
    pad = ((0, 0), (1, 0), (0, 0), (0, 0), (0, 0))
    k_band = jnp.concatenate([jnp.pad(kb[:, :-1], pad), kb], axis=2)
    v_band = jnp.concatenate([jnp.pad(vb[:, :-1], pad), vb], axis=2)
    scores = jnp.einsum('bnqkgd,bnskd->bnkgqs', qb, k_band,
                        preferred_element_type=jnp.float32) * scale
    i = jnp.arange(BLOCK)[:, None]
    j = jnp.arange(2 * BLOCK)[None, :]
    dist = BLOCK + i - j
    blk = jnp.arange(nb)[:, None, None]
    valid = (dist >= 0) & (dist <= WINDOW) & ((blk > 0) | (j >= BLOCK))
    p = sink_probs(scores, dist.astype(jnp.float32), valid[:, None, None], sinks)
    o = jnp.einsum('bnkgqs,bnskd->bnqkgd', p.astype(v.dtype), v_band)
    return o.reshape(B, T, ATTN_W)


def attn_sample(q, k, v, k_buf, v_buf, sinks):
    N, T = q.shape[0], q.shape[1]
    scale = 1.0 / math.sqrt(HEAD_DIM)
    k_all = jnp.concatenate([k_buf.astype(k.dtype), k], axis=1)
    v_all = jnp.concatenate([v_buf.astype(v.dtype), v], axis=1)
    qg = q.reshape(N, T, N_KV_HEADS, GROUP, HEAD_DIM)
    scores = jnp.einsum('nqkgd,nskd->nkgqs', qg, k_all,
                        preferred_element_type=jnp.float32) * scale
    i = jnp.arange(T)[:, None]
    j = jnp.arange(WINDOW + T)[None, :]
    dist = WINDOW + i - j
    valid = (dist >= 0) & (dist <= WINDOW)
    p = sink_probs(scores, dist.astype(jnp.float32), valid, sinks)
    o = jnp.einsum('nkgqs,nskd->nqkgd', p.astype(v.dtype), v_all)
    return o.reshape(N, T, ATTN_W), k_all[:, -WINDOW:], v_all[:, -WINDOW:]


def causal_depthwise_conv(u, hist, w, b):
    full = jnp.concatenate([hist.astype(u.dtype), u], axis=1)
    y = lax.conv_general_dilated(full, w.astype(u.dtype)[:, None, :], window_strides=(1,),
                                 padding='VALID', dimension_numbers=('NWC', 'WIO', 'NWC'),
                                 feature_group_count=CONV_CH)
    return y + b.astype(u.dtype), full[:, -CONV_HIST:]


def layer(x, conv_hist, k_buf, v_buf, w, prompt):
    (norm_g, w_in, b_in, conv_w, conv_b, conv_ln_g, conv_ln_b, w_conv_proj,
     q_norm_g, k_norm_g, sinks, w_attn_proj, w_out) = w
    B, T = x.shape[0], x.shape[1]
    xn = rms_norm(x, norm_g)
    z = xn @ w_in + b_in
    c_val, c_glu, c_gate, q, k, v, a_gate, g_a, g_b = split_columns(z)
    u = c_val * jax.nn.sigmoid(c_glu)
    cz, new_conv = causal_depthwise_conv(u, conv_hist, conv_w, conv_b)
    h_a = jax.nn.silu(layer_norm(cz, conv_ln_g, conv_ln_b)) * jax.nn.silu(c_gate)
    branch_a = h_a @ w_conv_proj
    q = rms_norm(q.reshape(B, T, N_HEADS, HEAD_DIM), q_norm_g)
    k = rms_norm(k.reshape(B, T, N_KV_HEADS, HEAD_DIM), k_norm_g)
    v = v.reshape(B, T, N_KV_HEADS, HEAD_DIM)
    if prompt:
        o = attn_prompt(q, k, v, sinks)
        new_k, new_v = k[:, -WINDOW:], v[:, -WINDOW:]
    else:
        o, new_k, new_v = attn_sample(q, k, v, k_buf, v_buf, sinks)
    branch_b = (o * jax.nn.silu(a_gate)) @ w_attn_proj
    merged = jax.nn.sigmoid(g_a) * branch_a + jax.nn.sigmoid(g_b) * branch_b
    y = x + merged @ w_out
    return y, new_conv, new_k, new_v


def setup_inputs(seed: int = 0) -> dict:
    key = jax.random.key(seed)
    ks = jax.random.split(key, 20)
    f32 = jnp.float32
    nrm = lambda k, shape, s: jax.random.normal(k, shape, f32) * s
    return {
        "x_prompt": nrm(ks[0], (BATCH, SEQ, D_MODEL), 1.0),
        "x_sample": nrm(ks[1], (DEC_BATCH, DEC_SEQ, D_MODEL), 1.0),
        "cache_k": nrm(ks[2], (DEPTH, DEC_BATCH, WINDOW, N_KV_HEADS, HEAD_DIM), 1.0),
        "cache_v": nrm(ks[3], (DEPTH, DEC_BATCH, WINDOW, N_KV_HEADS, HEAD_DIM), 1.0),
        "state_conv": nrm(ks[4], (DEPTH, DEC_BATCH, CONV_HIST, CONV_CH), 0.5),
        "norm_g": 1.0 + nrm(ks[5], (DEPTH, D_MODEL), 0.02),
        "w_in": nrm(ks[6], (DEPTH, D_MODEL, N_IN), D_MODEL ** -0.5),
        "b_in": nrm(ks[7], (DEPTH, N_IN), 0.02),
        "conv_w": nrm(ks[8], (DEPTH, CONV_WIDTH, CONV_CH), CONV_WIDTH ** -0.5),
        "conv_b": nrm(ks[9], (DEPTH, CONV_CH), 0.02),
        "conv_ln_g": 1.0 + nrm(ks[10], (DEPTH, CONV_CH), 0.02),
        "conv_ln_b": nrm(ks[11], (DEPTH, CONV_CH), 0.02),
        "w_conv_proj": nrm(ks[12], (DEPTH, CONV_CH, D_MODEL), CONV_CH ** -0.5),
        "q_norm_g": 1.0 + nrm(ks[13], (DEPTH, HEAD_DIM), 0.02),
        "k_norm_g": 1.0 + nrm(ks[14], (DEPTH, HEAD_DIM), 0.02),
        "sinks": nrm(ks[15], (DEPTH, N_HEADS), 0.5),
        "w_attn_proj": nrm(ks[16], (DEPTH, ATTN_W, D_MODEL), ATTN_W ** -0.5),
        "w_out": nrm(ks[17], (DEPTH, D_MODEL, D_MODEL), D_MODEL ** -0.5),
    }


def reference(x_prompt, x_sample, cache_k, cache_v, state_conv, norm_g, w_in, b_in, conv_w,
              conv_b, conv_ln_g, conv_ln_b, w_conv_proj, q_norm_g, k_norm_g, sinks,
              w_attn_proj, w_out):
    yp, ys = x_prompt, x_sample
    pk, pv, pc, sk, sv, sc = [], [], [], [], [], []
    zero_hist = jnp.zeros((x_prompt.shape[0], CONV_HIST, CONV_CH), x_prompt.dtype)
    for l in range(DEPTH):
        w = (norm_g[l], w_in[l], b_in[l], conv_w[l], conv_b[l], conv_ln_g[l], conv_ln_b[l],
             w_conv_proj[l], q_norm_g[l], k_norm_g[l], sinks[l], w_attn_proj[l], w_out[l])
        yp, c_p, k_p, v_p = layer(yp, zero_hist, None, None, w, True)
        ys, c_s, k_s, v_s = layer(ys, state_conv[l], cache_k[l], cache_v[l], w, False)
        pk.append(k_p); pv.append(v_p); pc.append(c_p)
        sk.append(k_s); sv.append(v_s); sc.append(c_s)
    return (yp, ys, jnp.stack(pk), jnp.stack(pv), jnp.stack(pc),
            jnp.stack(sk), jnp.stack(sv), jnp.stack(sc))
```

```python
import functools
import math

import jax
import jax.numpy as jnp
import numpy as np
from jax import lax
from jax.experimental import pallas as pl
from jax.experimental.pallas import tpu as pltpu

D_MODEL = 1024
N_HEADS = 16
N_KV = 4
GROUP = 4
HEAD_DIM = 64
KV_W = N_KV * HEAD_DIM
WINDOW = 128
CONV_W = 31
CONV_HIST = CONV_W - 1
EPS = 1e-6
CHUNK = 128
HALO = 32
LANES = 128
ROW_STRIDE = 4
VMEM_LIMIT = 56 * 1024 * 1024

_C_VAL, _C_GLU, _C_GATE, _Q, _K, _V, _A_GATE, _G_A, _G_B, _N_IN = (
    0, 1024, 2048, 3072, 4096, 4352, 4608, 5632, 6656, 7680)

F32 = jnp.float32
BF16 = jnp.bfloat16


def _slot_head(p):
    return 2 * (p // 8) + (p % 2), (p % 8) // 2


_SLOT_HEADS = [kv * GROUP + g for kv, g in map(_slot_head, range(N_HEADS))]
_SLOPES = [2.0 ** (-8.0 * (h + 1) / N_HEADS) for h in range(N_HEADS)]


def _permute_heads_cols(w):
    lead = w.shape[:-1]
    w = w.reshape(lead + (2, 2, GROUP, HEAD_DIM))
    w = jnp.swapaxes(w, -3, -2)
    return w.reshape(lead + (N_HEADS * HEAD_DIM,))


def _prompt_bias_table():
    i = np.arange(CHUNK)[:, None]
    j = np.arange(2 * CHUNK)[None, :]
    dist = (CHUNK + i - j).astype(np.float64)
    valid = (dist >= 0) & (dist <= WINDOW)
    out = np.empty((2, N_HEADS, CHUNK, 2 * CHUNK), np.float32)
    for first in range(2):
        v = valid & ((j >= CHUNK) if first else True)
        for p in range(N_HEADS):
            out[first, p] = np.where(v, _SLOPES[_SLOT_HEADS[p]] * dist, np.inf)
    return out.reshape(2 * N_HEADS * CHUNK, 2 * CHUNK)


def _sample_row_slot(r):
    t, half, c = r // 16, (r % 16) // 8, r % 8
    return t, 2 * c + half


def _sample_bias_table(dec_seq):
    rows = dec_seq * N_HEADS
    out = np.full((rows, 2 * CHUNK), np.inf, np.float32)
    j = np.arange(2 * CHUNK)
    for r in range(rows):
        t, p = _sample_row_slot(r)
        dist = WINDOW + t - j
        valid = (dist >= 0) & (dist <= WINDOW) & (j < WINDOW + dec_seq)
        out[r] = np.where(valid, _SLOPES[_SLOT_HEADS[p]] * dist, np.inf)
    return out


def _sample_masks():
    c = np.arange(8)[:, None]
    lane = np.arange(LANES)[None, :]
    low, high = lane < HEAD_DIM, lane >= HEAD_DIM
    return np.stack([(c < 4) & low, (c < 4) & high, (c >= 4) & low, (c >= 4) & high]).astype(np.float32)


def _head_sum_matrix():
    i = np.arange(KV_W)
    return (i[:, None] // HEAD_DIM == i[None, :] // HEAD_DIM).astype(np.float32)


def _sigmoid(x):
    return jax.nn.sigmoid(x)


def _silu(x):
    return x * _sigmoid(x)


def _dot(a, b):
    return jnp.dot(a, b, preferred_element_type=F32)


def _dot_t(a, b):
    return lax.dot_general(a, b, (((1,), (1,)), ((), ())), preferred_element_type=F32)


def _rms_rows(x, g):
    ms = jnp.mean(x * x, axis=-1, keepdims=True)
    return x * lax.rsqrt(ms + EPS) * g


def _layer_norm_rows(x, g, b):
    mu = jnp.mean(x, axis=-1, keepdims=True)
    d = x - mu
    var = jnp.mean(d * d, axis=-1, keepdims=True)
    return d * lax.rsqrt(var + EPS) * g + b


def _proj(h, win_ref, bin_ref, a, b):
    return _dot(h, win_ref[:, a:b]) + bin_ref[:, a:b]


def _head_norm(x, e_ref):
    sq = (x * x).astype(BF16)
    n = x.shape[-1] // KV_W
    sums = [_dot(sq[:, i * KV_W:(i + 1) * KV_W], e_ref[...]) for i in range(n)]
    s = sums[0] if n == 1 else jnp.concatenate(sums, axis=-1)
    return x * lax.rsqrt(s * (1.0 / HEAD_DIM) + EPS)


def _prompt_kernel(sinks_ref, x_ref, ng_ref, win_ref, bin_ref, cw_ref, cb_ref, lng_ref, lnb_ref, wcp_ref,
                   gq_ref, gk_ref, e_ref, bias_ref, wap_ref, wout_ref,
                   y_ref, knew_ref, vnew_ref, convnew_ref,
                   u_sc, cz_sc, kprev_sc, vprev_sc, *, tile):
    j = pl.program_id(1)
    n_chunks = tile // CHUNK

    @pl.when(j == 0)
    def _():
        u_sc[:, 0:HALO, :] = jnp.zeros((8, HALO, LANES), F32)
        kprev_sc[...] = jnp.zeros((CHUNK, KV_W), F32)
        vprev_sc[...] = jnp.zeros((CHUNK, KV_W), F32)

    @pl.when(j > 0)
    def _():
        u_sc[:, 0:HALO, :] = u_sc[:, tile:tile + HALO, :]

    lane = lax.broadcasted_iota(jnp.int32, (CHUNK, LANES), 1)
    low_half = lane < HEAD_DIM

    def chunk(c, carry):
        r0 = pl.multiple_of(c * CHUNK, CHUNK)
        first = jnp.logical_and(j == 0, c == 0).astype(jnp.int32)
        x = x_ref[0, pl.ds(r0, CHUNK), :]
        h = _rms_rows(x, ng_ref[...]).astype(BF16)

        u = _proj(h, win_ref, bin_ref, _C_VAL, _C_GLU) * _sigmoid(_proj(h, win_ref, bin_ref, _C_GLU, _C_GATE))
        for blk in range(8):
            u_sc[blk, pl.ds(HALO + r0, CHUNK), :] = u[:, blk * LANES:(blk + 1) * LANES]
        for blk in range(8):
            for t0 in range(0, CHUNK, 8 * ROW_STRIDE):
                for kk in range(ROW_STRIDE):
                    acc = None
                    for tap in range(CONV_W):
                        start = r0 + (HALO - CONV_HIST + t0 + kk + tap)
                        win = u_sc[blk, pl.ds(start, 8, stride=ROW_STRIDE), :]
                        term = win * cw_ref[tap:tap + 1, blk * LANES:(blk + 1) * LANES]
                        acc = term if acc is None else acc + term
                    cz_sc[blk, pl.ds(t0 + kk, 8, stride=ROW_STRIDE), :] = acc
        cz = jnp.concatenate([cz_sc[blk] for blk in range(8)], axis=-1) + cb_ref[...]
        yln = _layer_norm_rows(cz, lng_ref[...], lnb_ref[...])
        h_a = _silu(yln) * _silu(_proj(h, win_ref, bin_ref, _C_GATE, _Q))
        branch_a = _dot(h_a.astype(BF16), wcp_ref[...])

        q = _head_norm(_proj(h, win_ref, bin_ref, _Q, _K), e_ref)
        kv = _proj(h, win_ref, bin_ref, _K, _A_GATE)
        kn = _head_norm(kv[:, :KV_W], e_ref) * gk_ref[...]
        v = kv[:, KV_W:]
        kb = jnp.concatenate([kprev_sc[...].astype(BF16), kn.astype(BF16)], axis=0)
        vb = jnp.concatenate([vprev_sc[...].astype(BF16), v.astype(BF16)], axis=0)
        q_lo, q_hi = [], []
        for cc in range(8):
            qc = q[:, cc * LANES:(cc + 1) * LANES]
            q_lo.append((qc * gq_ref[0:1, cc * LANES:(cc + 1) * LANES]).astype(BF16))
            q_hi.append((qc * gq_ref[1:2, cc * LANES:(cc + 1) * LANES]).astype(BF16))
        outs = []
        for half_kv in range(2):
            qs = []
            for cc in range(4 * half_kv, 4 * half_kv + 4):
                qs += [q_lo[cc], q_hi[cc]]
            qstack = jnp.concatenate(qs, axis=0)
            s = _dot_t(qstack, kb[:, half_kv * LANES:(half_kv + 1) * LANES])
            probs, rinv = [], []
            for sl in range(8):
                p = 8 * half_kv + sl
                boff = pl.multiple_of(first * (N_HEADS * CHUNK) + p * CHUNK, CHUNK)
                lg = s[sl * CHUNK:(sl + 1) * CHUNK] - bias_ref[pl.ds(boff, CHUNK), :]
                sink = sinks_ref[p]
                m = jnp.maximum(jnp.max(lg, axis=-1, keepdims=True), sink)
                pe = jnp.exp(lg - m)
                den = jnp.sum(pe, axis=-1, keepdims=True) + jnp.exp(sink - m)
                probs.append(pe.astype(BF16))
                rinv.append(1.0 / den)
            o = _dot(jnp.concatenate(probs, axis=0), vb[:, half_kv * LANES:(half_kv + 1) * LANES])
            for cc in range(4):
                o_lo = o[(2 * cc) * CHUNK:(2 * cc + 1) * CHUNK] * rinv[2 * cc]
                o_hi = o[(2 * cc + 1) * CHUNK:(2 * cc + 2) * CHUNK] * rinv[2 * cc + 1]
                outs.append(jnp.where(low_half, o_lo, o_hi))
        attn = jnp.concatenate(outs, axis=-1)
        o_g = attn * _silu(_proj(h, win_ref, bin_ref, _A_GATE, _G_A))
        branch_b = _dot(o_g.astype(BF16), wap_ref[...])

        merged = (_sigmoid(_proj(h, win_ref, bin_ref, _G_A, _G_B)) * branch_a
                  + _sigmoid(_proj(h, win_ref, bin_ref, _G_B, _N_IN)) * branch_b)
        y_ref[0, pl.ds(r0, CHUNK), :] = x + _dot(merged.astype(BF16), wout_ref[...])

        kprev_sc[...] = kn
        vprev_sc[...] = v
        knew_ref[0] = kn
        vnew_ref[0] = v
        return carry

    lax.fori_loop(0, n_chunks, chunk, 0)

    @pl.when(j == pl.num_programs(1) - 1)
    def _():
        for blk in range(8):
            convnew_ref[0, :, blk * LANES:(blk + 1) * LANES] = u_sc[blk, HALO + tile - CONV_HIST:HALO + tile, :]


def _const_spec(shape):
    nd = len(shape)
    return pl.BlockSpec(shape, lambda *_: (0,) * nd, pipeline_mode=pl.Buffered(1))


def _prompt_call(x, sinks_slot, wts, tile):
    batch, seq, _ = x.shape
    assert seq % tile == 0 and tile % CHUNK == 0
    consts = [wts["ng"], wts["win"], wts["bin"], wts["cw"], wts["cb"], wts["lng"], wts["lnb"], wts["wcp"],
              wts["gq"], wts["gk"], wts["e"], wts["bias_p"], wts["wap"], wts["wout"]]
    grid_spec = pltpu.PrefetchScalarGridSpec(
        num_scalar_prefetch=1,
        grid=(batch, seq // tile),
        in_specs=[pl.BlockSpec((1, tile, D_MODEL), lambda b, j, *_: (b, j, 0))]
        + [_const_spec(c.shape) for c in consts],
        out_specs=[pl.BlockSpec((1, tile, D_MODEL), lambda b, j, *_: (b, j, 0)),
                   pl.BlockSpec((1, CHUNK, KV_W), lambda b, j, *_: (b, 0, 0)),
                   pl.BlockSpec((1, CHUNK, KV_W), lambda b, j, *_: (b, 0, 0)),
                   pl.BlockSpec((1, CONV_HIST, D_MODEL), lambda b, j, *_: (b, 0, 0))],
        scratch_shapes=[pltpu.VMEM((8, HALO + tile, LANES), F32),
                        pltpu.VMEM((8, CHUNK, LANES), F32),
                        pltpu.VMEM((CHUNK, KV_W), F32),
                        pltpu.VMEM((CHUNK, KV_W), F32)],
    )
    return pl.pallas_call(
        functools.partial(_prompt_kernel, tile=tile),
        grid_spec=grid_spec,
        out_shape=[jax.ShapeDtypeStruct((batch, seq, D_MODEL), F32),
                   jax.ShapeDtypeStruct((batch, CHUNK, KV_W), F32),
                   jax.ShapeDtypeStruct((batch, CHUNK, KV_W), F32),
                   jax.ShapeDtypeStruct((batch, CONV_HIST, D_MODEL), F32)],
        compiler_params=pltpu.CompilerParams(dimension_semantics=("arbitrary", "arbitrary"),
                                             vmem_limit_bytes=VMEM_LIMIT),
        name="prompt_layer",
    )(sinks_slot, x, *consts)


def _rows_t_major(x2_ref, dec_seq):
    return jnp.concatenate([x2_ref[:, t * D_MODEL:(t + 1) * D_MODEL] for t in range(dec_seq)], axis=0)


def _sample_in_kernel(x_ref, ng_ref, win_ref, bin_ref, gq_ref, gk_ref, e_ref,
                      u_ref, cgs_ref, q_ref, k_ref, v_ref, ags_ref, sga_ref, sgb_ref, *, dec_seq, n_seq):
    x = _rows_t_major(x_ref, dec_seq)
    h = _rms_rows(x, ng_ref[...]).astype(BF16)

    def put(ref, val):
        for t in range(dec_seq):
            ref[t] = val[t * n_seq:(t + 1) * n_seq]

    put(u_ref, _proj(h, win_ref, bin_ref, _C_VAL, _C_GLU) * _sigmoid(_proj(h, win_ref, bin_ref, _C_GLU, _C_GATE)))
    put(cgs_ref, _silu(_proj(h, win_ref, bin_ref, _C_GATE, _Q)))
    put(q_ref, _head_norm(_proj(h, win_ref, bin_ref, _Q, _K), e_ref) * (gq_ref[0:1, :] + gq_ref[1:2, :]))
    kv = _proj(h, win_ref, bin_ref, _K, _A_GATE)
    put(k_ref, _head_norm(kv[:, :KV_W], e_ref) * gk_ref[...])
    put(v_ref, kv[:, KV_W:])
    put(ags_ref, _silu(_proj(h, win_ref, bin_ref, _A_GATE, _G_A)))
    put(sga_ref, _sigmoid(_proj(h, win_ref, bin_ref, _G_A, _G_B)))
    put(sgb_ref, _sigmoid(_proj(h, win_ref, bin_ref, _G_B, _N_IN)))


def _sample_in_call(x2, wts, dec_seq):
    n_seq = x2.shape[0]
    consts = [wts["ng"], wts["win"], wts["bin"], wts["gq"], wts["gk"], wts["e"]]
    wide = jax.ShapeDtypeStruct((dec_seq, n_seq, D_MODEL), F32)
    narrow = jax.ShapeDtypeStruct((dec_seq, n_seq, KV_W), F32)
    return pl.pallas_call(
        functools.partial(_sample_in_kernel, dec_seq=dec_seq, n_seq=n_seq),
        out_shape=[wide, wide, wide, narrow, narrow, wide, wide, wide],
        compiler_params=pltpu.CompilerParams(vmem_limit_bytes=VMEM_LIMIT),
        name="sample_in_proj",
    )(x2, *consts)


def _sample_mix_kernel(sinkcol_ref, u_ref, q_ref, k_ref, v_ref, state_ref, ck_ref, cv_ref, cw_ref, cb_ref,
                       bias_ref, mask_ref,
                       cz_ref, o_ref, convnew_ref, knew_ref, vnew_ref,
                       kfull_sc, vfull_sc, *, dec_seq, n_blk):
    @pl.when(pl.program_id(0) == 0)
    def _():
        kfull_sc[...] = jnp.zeros(kfull_sc.shape, F32)
        vfull_sc[...] = jnp.zeros(vfull_sc.shape, F32)

    for t in range(dec_seq):
        acc = None
        for s in range(t, CONV_HIST):
            term = state_ref[:, s * D_MODEL:(s + 1) * D_MODEL] * cw_ref[s - t:s - t + 1, :]
            acc = term if acc is None else acc + term
        for t2 in range(t + 1):
            tap = CONV_HIST + t2 - t
            acc = acc + u_ref[t2] * cw_ref[tap:tap + 1, :]
        cz_ref[t] = acc + cb_ref[...]
    keep = CONV_HIST - dec_seq
    convnew_ref[:, 0:keep * D_MODEL] = state_ref[:, dec_seq * D_MODEL:CONV_HIST * D_MODEL]
    for t in range(dec_seq):
        convnew_ref[:, (keep + t) * D_MODEL:(keep + t + 1) * D_MODEL] = u_ref[t]

    sub = lax.broadcasted_iota(jnp.int32, (8, LANES), 0)
    lane2 = lax.broadcasted_iota(jnp.int32, (8, 2 * LANES), 1)
    low2 = (lane2 % LANES) < HEAD_DIM
    first_cols = sub < 4

    def seq_body(n, carry):
        kfull_sc[0:WINDOW, :] = ck_ref[n]
        vfull_sc[0:WINDOW, :] = cv_ref[n]
        for t in range(dec_seq):
            kfull_sc[WINDOW + t:WINDOW + t + 1, :] = k_ref[t, n:n + 1, :]
            vfull_sc[WINDOW + t:WINDOW + t + 1, :] = v_ref[t, n:n + 1, :]
        knew_ref[n] = kfull_sc[dec_seq:dec_seq + WINDOW, :]
        vnew_ref[n] = vfull_sc[dec_seq:dec_seq + WINDOW, :]

        qrows = []
        for t in range(dec_seq):
            row = q_ref[t, n:n + 1, :]
            cols = jnp.zeros((8, LANES), F32)
            for cc in range(8):
                cols = jnp.where(sub == cc, row[:, cc * LANES:(cc + 1) * LANES], cols)
            a = jnp.concatenate([cols * mask_ref[0], cols * mask_ref[1]], axis=0)
            b = jnp.concatenate([cols * mask_ref[2], cols * mask_ref[3]], axis=0)
            qrows.append(jnp.concatenate([a, b], axis=1))
        qpad = jnp.concatenate(qrows, axis=0).astype(BF16)
        lg = _dot_t(qpad, kfull_sc[...].astype(BF16)) - bias_ref[...]
        sink = sinkcol_ref[:, 0:1]
        m = jnp.maximum(jnp.max(lg, axis=-1, keepdims=True), sink)
        pe = jnp.exp(lg - m)
        den = jnp.sum(pe, axis=-1, keepdims=True) + jnp.exp(sink - m)
        o = _dot(pe.astype(BF16), vfull_sc[...].astype(BF16)) * (1.0 / den)
        for t in range(dec_seq):
            both = jnp.where(low2, o[16 * t:16 * t + 8], o[16 * t + 8:16 * t + 16])
            cols = jnp.where(first_cols, both[:, 0:LANES], both[:, LANES:2 * LANES])
            for cc in range(8):
                o_ref[t, n:n + 1, cc * LANES:(cc + 1) * LANES] = cols[cc:cc + 1, :]
        return carry

    for n in range(n_blk):
        seq_body(n, 0)


def _sample_mix_call(u, q, k, v, state2, ck, cv, sinkcol, wts, dec_seq, n_blk):
    n_seq = state2.shape[0]
    assert n_seq % n_blk == 0
    rows = dec_seq * N_HEADS

    def tmaj(width):
        return pl.BlockSpec((dec_seq, n_blk, width), lambda g: (0, g, 0))

    def full(arr):
        nd = arr.ndim
        return pl.BlockSpec(arr.shape, lambda g: (0,) * nd)

    cache_spec = pl.BlockSpec((n_blk, WINDOW, KV_W), lambda g: (g, 0, 0))
    state_spec = pl.BlockSpec((n_blk, CONV_HIST * D_MODEL), lambda g: (g, 0))
    return pl.pallas_call(
        functools.partial(_sample_mix_kernel, dec_seq=dec_seq, n_blk=n_blk),
        grid=(n_seq // n_blk,),
        in_specs=[full(sinkcol), tmaj(D_MODEL), tmaj(D_MODEL), tmaj(KV_W), tmaj(KV_W), state_spec, cache_spec,
                  cache_spec, full(wts["cw"]), full(wts["cb"]), full(wts["bias_s"]), full(wts["mask_s"])],
        out_specs=[tmaj(D_MODEL), tmaj(D_MODEL), state_spec, cache_spec, cache_spec],
        out_shape=[jax.ShapeDtypeStruct((dec_seq, n_seq, D_MODEL), F32),
                   jax.ShapeDtypeStruct((dec_seq, n_seq, D_MODEL), F32),
                   jax.ShapeDtypeStruct(state2.shape, F32),
                   jax.ShapeDtypeStruct(ck.shape, F32),
                   jax.ShapeDtypeStruct(cv.shape, F32)],
        scratch_shapes=[pltpu.VMEM((2 * CHUNK, KV_W), F32), pltpu.VMEM((2 * CHUNK, KV_W), F32)],
        compiler_params=pltpu.CompilerParams(dimension_semantics=("arbitrary",), vmem_limit_bytes=VMEM_LIMIT),
        name="sample_conv_attn",
    )(sinkcol, u, q, k, v, state2, ck, cv, wts["cw"], wts["cb"], wts["bias_s"], wts["mask_s"])


def _sample_out_kernel(x_ref, cz_ref, cgs_ref, o_ref, ags_ref, sga_ref, sgb_ref, lng_ref, lnb_ref, wcp_ref,
                       wap_ref, wout_ref, y_ref, *, dec_seq, n_seq):
    rows = dec_seq * n_seq

    def flat(ref):
        return ref[...].reshape(rows, ref.shape[-1])

    x = _rows_t_major(x_ref, dec_seq)
    h_a = _silu(_layer_norm_rows(flat(cz_ref), lng_ref[...], lnb_ref[...])) * flat(cgs_ref)
    branch_a = _dot(h_a.astype(BF16), wcp_ref[...])
    branch_b = _dot((flat(o_ref) * flat(ags_ref)).astype(BF16), wap_ref[...])
    merged = flat(sga_ref) * branch_a + flat(sgb_ref) * branch_b
    y = x + _dot(merged.astype(BF16), wout_ref[...])
    for t in range(dec_seq):
        y_ref[:, t * D_MODEL:(t + 1) * D_MODEL] = y[t * n_seq:(t + 1) * n_seq]


def _sample_out_call(x2, cz, cgs, o, ags, sga, sgb, wts, dec_seq):
    n_seq = x2.shape[0]
    return pl.pallas_call(
        functools.partial(_sample_out_kernel, dec_seq=dec_seq, n_seq=n_seq),
        out_shape=jax.ShapeDtypeStruct(x2.shape, F32),
        compiler_params=pltpu.CompilerParams(vmem_limit_bytes=VMEM_LIMIT),
        name="sample_out_proj",
    )(x2, cz, cgs, o, ags, sga, sgb, wts["lng"], wts["lnb"], wts["wcp"], wts["wap"], wts["wout"])


def _layer_weights(l, dec_seq, norm_g, w_in, b_in, conv_w, conv_b, conv_ln_g, conv_ln_b, w_conv_proj, q_norm_g,
                   k_norm_g, sinks, w_attn_proj, w_out):
    def perm_in(w):
        return jnp.concatenate([w[..., :_Q], _permute_heads_cols(w[..., _Q:_K]), w[..., _K:_A_GATE],
                                _permute_heads_cols(w[..., _A_GATE:_G_A]), w[..., _G_A:]], axis=-1)

    scale = 1.0 / math.sqrt(HEAD_DIM)
    lane = np.arange(D_MODEL) % LANES
    halves = jnp.asarray(np.stack([lane < HEAD_DIM, lane >= HEAD_DIM]).astype(np.float32))
    gq = jnp.tile(q_norm_g[l].astype(F32), N_HEADS)[None, :] * scale * halves
    wap = _permute_heads_cols(w_attn_proj[l].T).T
    return dict(
        ng=norm_g[l][None, :], win=perm_in(w_in[l]).astype(BF16), bin=perm_in(b_in[l])[None, :],
        cw=conv_w[l], cb=conv_b[l][None, :], lng=conv_ln_g[l][None, :], lnb=conv_ln_b[l][None, :],
        wcp=w_conv_proj[l].astype(BF16), gq=gq, gk=jnp.tile(k_norm_g[l].astype(F32), N_KV)[None, :],
        e=jnp.asarray(_head_sum_matrix(), BF16), bias_p=jnp.asarray(_prompt_bias_table()),
        wap=wap.astype(BF16), wout=w_out[l].astype(BF16),
        bias_s=jnp.asarray(_sample_bias_table(dec_seq)), mask_s=jnp.asarray(_sample_masks()),
    )


def kernel(x_prompt, x_sample, cache_k, cache_v, state_conv, norm_g, w_in, b_in, conv_w, conv_b, conv_ln_g,
           conv_ln_b, w_conv_proj, q_norm_g, k_norm_g, sinks, w_attn_proj, w_out):
    depth = w_in.shape[0]
    batch, seq, _ = x_prompt.shape
    n_seq, dec_seq, _ = x_sample.shape
    tile = 512 if seq % 512 == 0 else CHUNK
    yp, ys = x_prompt, x_sample
    pk, pv, pc, sk, sv, sc = [], [], [], [], [], []
    for l in range(depth):
        wts = _layer_weights(l, dec_seq, norm_g, w_in, b_in, conv_w, conv_b, conv_ln_g, conv_ln_b, w_conv_proj,
                             q_norm_g, k_norm_g, sinks, w_attn_proj, w_out)
        sinks_slot = sinks[l][jnp.asarray(_SLOT_HEADS)].astype(F32)

        yp, k_p, v_p, c_p = _prompt_call(yp, sinks_slot, wts, tile)
        pk.append(k_p.reshape(batch, WINDOW, N_KV, HEAD_DIM))
        pv.append(v_p.reshape(batch, WINDOW, N_KV, HEAD_DIM))
        pc.append(c_p)

        x2 = ys.reshape(n_seq, dec_seq * D_MODEL)
        u, cgs, q, k, v, ags, sga, sgb = _sample_in_call(x2, wts, dec_seq)
        row_slot = [_sample_row_slot(r)[1] for r in range(dec_seq * N_HEADS)]
        sinkcol = jnp.broadcast_to(sinks_slot[jnp.asarray(row_slot)][:, None], (dec_seq * N_HEADS, LANES))
        cz, o, c_s, k_s, v_s = _sample_mix_call(
            u, q, k, v, state_conv[l].reshape(n_seq, CONV_HIST * D_MODEL),
            cache_k[l].reshape(n_seq, WINDOW, KV_W), cache_v[l].reshape(n_seq, WINDOW, KV_W),
            sinkcol, wts, dec_seq, n_blk=8)
        ys = _sample_out_call(x2, cz, cgs, o, ags, sga, sgb, wts, dec_seq).reshape(n_seq, dec_seq, D_MODEL)
        sk.append(k_s.reshape(n_seq, WINDOW, N_KV, HEAD_DIM))
        sv.append(v_s.reshape(n_seq, WINDOW, N_KV, HEAD_DIM))
        sc.append(c_s.reshape(n_seq, CONV_HIST, D_MODEL))
    return (yp, ys, jnp.stack(pk), jnp.stack(pv), jnp.stack(pc), jnp.stack(sk), jnp.stack(sv), jnp.stack(sc))
```
